```python
import jax, jax.numpy as jnp
from jax import lax
import numpy as np

D_MODEL = 1024
BATCH = 8
SEQ = 2048
DEPTH = 2
DEC_BATCH = 32
DEC_SEQ = 8
PAST_LEN = 8192
PAGE_SIZE = 128

A_HEADS = 8
A_HEAD_DIM = 64
D_A = A_HEADS * A_HEAD_DIM
Q_BLOCK = 128
FORGET_BIAS = 3.0
B_HEADS = 4
B_KEY_DIM = 64
B_VAL_DIM = 128
D_BK = B_HEADS * B_KEY_DIM
D_B = B_HEADS * B_VAL_DIM
GLA_LOW_RANK = 16
GLA_TAU = 16.0
GLA_CHUNK = 64
D_C = D_MODEL // 2
CONV_W = 3

N_BRANCH = 3
EPS = 1e-6
SPLIT_SIZES = (D_A, D_A, D_A, A_HEADS, D_A,
               D_BK, D_BK, D_B, GLA_LOW_RANK, D_B,
               D_C, D_C, D_C, D_C,
               N_BRANCH * D_MODEL)
N_IN = sum(SPLIT_SIZES)

kernel_name = 'hybrid_fox_gla_shortconv_decode_step'


def rms_norm(x, g):
    xf = x.astype(jnp.float32)
    y = xf * lax.rsqrt(jnp.mean(xf * xf, axis=-1, keepdims=True) + EPS) * g.astype(jnp.float32)
    return y.astype(x.dtype)


def split_columns(z):
    idx = [int(i) for i in np.cumsum(SPLIT_SIZES)[:-1]]
    return jnp.split(z, idx, axis=-1)


def fox_attention(q, k, v, fq, fk, q_pos, k_pos):
    bsz, tq, h, dh = q.shape
    blk = min(Q_BLOCK, tq)
    nb = -(-tq // blk)
    pad = nb * blk - tq
    qp = jnp.pad(q, ((0, 0), (0, pad), (0, 0), (0, 0)))
    fqp = jnp.pad(fq, ((0, 0), (0, pad), (0, 0)))
    posp = jnp.pad(q_pos, (0, pad), mode='edge')
    qb = qp.reshape(bsz, nb, blk, h, dh).transpose(1, 0, 2, 3, 4)
    fqb = fqp.reshape(bsz, nb, blk, h).transpose(1, 0, 2, 3)
    pb = posp.reshape(nb, blk)
    fk_t = fk.transpose(0, 2, 1)
    scale = dh ** -0.5

    def one_block(args):
        qi, fi, pi = args
        s = jnp.einsum('bqhd,bkhd->bhqk', qi, k, preferred_element_type=jnp.float32) * scale
        s = s + fi.transpose(0, 2, 1)[..., :, None] - fk_t[..., None, :]
        mask = k_pos[None, :] <= pi[:, None]
        s = jnp.where(mask, s, -jnp.inf)
        p = jax.nn.softmax(s, axis=-1)
        return jnp.einsum('bhqk,bkhd->bqhd', p.astype(v.dtype), v)

    out = lax.map(one_block, (qb, fqb, pb))
    return out.transpose(1, 0, 2, 3, 4).reshape(bsz, nb * blk, h, dh)[:, :tq]


def gla_chunked(q, k, v, log_a, s0):
    bsz, t, h, dk = q.shape
    dv = v.shape[-1]
    c = min(GLA_CHUNK, t)
    n = -(-t // c)
    pad = n * c - t

    def to_chunks(a):
        a = jnp.pad(a.astype(jnp.float32), ((0, 0), (0, pad), (0, 0), (0, 0)))
        return a.reshape(bsz, n, c, h, a.shape[-1]).transpose(1, 0, 3, 2, 4)

    qc = to_chunks(q * (dk ** -0.5))
    kc = to_chunks(k)
    vc = to_chunks(v)
    ac = to_chunks(log_a)
    causal = jnp.tril(jnp.ones((c, c), dtype=bool))

    def step(state, inp):
        qi, ki, vi, ai = inp
        b = jnp.cumsum(ai, axis=2)
        o_inter = jnp.einsum('bhtd,bhde->bhte', qi * jnp.exp(b), state)
        diff = b[:, :, :, None, :] - b[:, :, None, :, :]
        decay = jnp.exp(jnp.where(causal[:, :, None], diff, -jnp.inf))
        att = jnp.einsum('bhtd,bhsd,bhtsd->bhts', qi, ki, decay)
        o = o_inter + jnp.einsum('bhts,bhse->bhte', att, vi)
        b_last = b[:, :, -1:, :]
        new_state = (jnp.exp(b_last[:, :, 0, :])[..., None] * state
                     + jnp.einsum('bhsd,bhse->bhde', ki * jnp.exp(b_last - b), vi))
        return new_state, o

    s_final, oc = lax.scan(step, s0.astype(jnp.float32), (qc, kc, vc, ac))
    o = oc.transpose(1, 0, 3, 2, 4).reshape(bsz, n * c, h, dv)[:, :t]
    return o, s_final


def short_conv(u, buf, w):
    t = u.shape[1]
    ext = jnp.concatenate([buf.astype(u.dtype), u], axis=1)
    y = w[0] * ext[:, 0:t]
    for i in range(1, CONV_W):
        y = y + w[i] * ext[:, i:i + t]
    return y, ext[:, -(CONV_W - 1):]


def hybrid_layer(x, kv_past, gla_state, conv_state, g_norm, w_in, b_forget, w_gla_up, b_gla,
                 g_gla, conv_w, w_branch_a, w_branch_b, w_branch_c, w_out):
    bsz, t, _ = x.shape
    h = rms_norm(x, g_norm)
    z = jnp.einsum('btd,dn->btn', h, w_in)
    (qa, ka, va, fa, ga, qb, kb, vb, lrb, gb, xc, bc, cc, gc, mg) = split_columns(z)

    qa = qa.reshape(bsz, t, A_HEADS, A_HEAD_DIM)
    ka = ka.reshape(bsz, t, A_HEADS, A_HEAD_DIM)
    va = va.reshape(bsz, t, A_HEADS, A_HEAD_DIM)
    logf = jax.nn.log_sigmoid(fa.astype(jnp.float32) + b_forget.astype(jnp.float32))
    if kv_past is None:
        offset = 0
        k_all, v_all, logf_all = ka, va, logf
    else:
        k_past, v_past, logf_past = kv_past
        offset = k_past.shape[1]
        k_all = jnp.concatenate([k_past.astype(ka.dtype), ka], axis=1)
        v_all = jnp.concatenate([v_past.astype(va.dtype), va], axis=1)
        logf_all = jnp.concatenate([logf_past.astype(jnp.float32), logf], axis=1)
    cum_f = jnp.cumsum(logf_all, axis=1)
    q_pos = offset + jnp.arange(t)
    k_pos = jnp.arange(offset + t)
    ya = fox_attention(qa, k_all, v_all, cum_f[:, offset:], cum_f, q_pos, k_pos)
    ya = ya.reshape(bsz, t, D_A) * jax.nn.silu(ga)

    qb = qb.reshape(bsz, t, B_HEADS, B_KEY_DIM)
    kb = kb.reshape(bsz, t, B_HEADS, B_KEY_DIM)
    vb = vb.reshape(bsz, t, B_HEADS, B_VAL_DIM)
    a_logit = jnp.einsum('btr,rk->btk', lrb, w_gla_up) + b_gla
    log_a = (jax.nn.log_sigmoid(a_logit.astype(jnp.float32)) / GLA_TAU).reshape(bsz, t, B_HEADS, B_KEY_DIM)
    ob, gla_new = gla_chunked(qb, kb, vb, log_a, gla_state)
    ob = rms_norm(ob, g_gla.reshape(B_HEADS, B_VAL_DIM)).astype(x.dtype)
    yb = ob.reshape(bsz, t, D_B) * jax.nn.silu(gb)

    u = cc * xc
    yconv, conv_new = short_conv(u, conv_state, conv_w)
    yc = bc * yconv * jax.nn.silu(gc)

    gates = jax.nn.sigmoid(mg.astype(jnp.float32)).astype(x.dtype).reshape(bsz, t, N_BRANCH, D_MODEL)
    merged = (gates[:, :, 0] * jnp.einsum('bte,ed->btd', ya, w_branch_a)
              + gates[:, :, 1] * jnp.einsum('bte,ed->btd', yb, w_branch_b)
              + gates[:, :, 2] * jnp.einsum('bte,ed->btd', yc, w_branch_c))
    out = x + jnp.einsum('btd,de->bte', merged, w_out)
    return out, ka, va, logf, gla_new, conv_new


def setup_inputs(seed: int = 0) -> dict:
    key = jax.random.key(seed)
    ks = jax.random.split(key, 24)
    f32 = jnp.float32
    n_pages = PAST_LEN // PAGE_SIZE
    n_used = DEC_BATCH * n_pages
    n_pool = (n_used * 5) // 4

    def nrm(k, shape, s):
        return jax.random.normal(k, shape, f32) * s

    x_prompt = nrm(ks[0], (BATCH, SEQ, D_MODEL), 1.0)
    x_sample = nrm(ks[1], (DEC_BATCH, DEC_SEQ, D_MODEL), 1.0)
    cache_k = nrm(ks[2], (DEPTH, n_pool, PAGE_SIZE, A_HEADS, A_HEAD_DIM), 1.0)
    cache_v = nrm(ks[3], (DEPTH, n_pool, PAGE_SIZE, A_HEADS, A_HEAD_DIM), 1.0)
    cache_logf = jax.nn.log_sigmoid(FORGET_BIAS + nrm(ks[4], (DEPTH, n_pool, PAGE_SIZE, A_HEADS), 1.0))
    state_gla = nrm(ks[5], (DEPTH, DEC_BATCH, B_HEADS, B_KEY_DIM, B_VAL_DIM), 0.5)
    state_conv = nrm(ks[6], (DEPTH, DEC_BATCH, CONV_W - 1, D_C), 1.0)
    page_table = jax.random.permutation(ks[7], n_pool)[:n_used].reshape(DEC_BATCH, n_pages).astype(jnp.int32)
    g_norm = 1.0 + nrm(ks[8], (DEPTH, D_MODEL), 0.05)
    w_in = nrm(ks[9], (DEPTH, D_MODEL, N_IN), D_MODEL ** -0.5)
    b_forget = FORGET_BIAS + nrm(ks[10], (DEPTH, A_HEADS), 0.1)
    w_gla_up = nrm(ks[11], (DEPTH, GLA_LOW_RANK, D_BK), GLA_LOW_RANK ** -0.5)
    b_gla = nrm(ks[12], (DEPTH, D_BK), 0.01)
    g_gla = 1.0 + nrm(ks[13], (DEPTH, D_B), 0.05)
    conv_w = nrm(ks[14], (DEPTH, CONV_W, D_C), CONV_W ** -0.5)
    w_branch_a = nrm(ks[15], (DEPTH, D_A, D_MODEL), D_A ** -0.5)
    w_branch_b = nrm(ks[16], (DEPTH, D_B, D_MODEL), D_B ** -0.5)
    w_branch_c = nrm(ks[17], (DEPTH, D_C, D_MODEL), D_C ** -0.5)
    w_out = nrm(ks[18], (DEPTH, D_MODEL, D_MODEL), D_MODEL ** -0.5)
    g_final = 1.0 + nrm(ks[19], (D_MODEL,), 0.05)
    return {'x_prompt': x_prompt, 'x_sample': x_sample,
            'cache_k': cache_k, 'cache_v': cache_v, 'cache_logf': cache_logf,
            'state_gla': state_gla, 'state_conv': state_conv, 'page_table': page_table,
            'g_norm': g_norm, 'w_in': w_in, 'b_forget': b_forget, 'w_gla_up': w_gla_up,
            'b_gla': b_gla, 'g_gla': g_gla, 'conv_w': conv_w, 'w_branch_a': w_branch_a,
            'w_branch_b': w_branch_b, 'w_branch_c': w_branch_c, 'w_out': w_out, 'g_final': g_final}


def reference(x_prompt, x_sample, cache_k, cache_v, cache_logf, state_gla, state_conv, page_table,
              g_norm, w_in, b_forget, w_gla_up, b_gla, g_gla, conv_w, w_branch_a, w_branch_b,
              w_branch_c, w_out, g_final):
    n_pages = page_table.shape[1]
    past_len = n_pages * PAGE_SIZE
    bp = x_prompt.shape[0]
    bs = x_sample.shape[0]
    xp, xs = x_prompt, x_sample
    kp_l, vp_l, fp_l, ks_l, vs_l, fs_l = [], [], [], [], [], []
    gp_l, gs_l, cp_l, cs_l = [], [], [], []
    for l in range(DEPTH):
        lw = (g_norm[l], w_in[l], b_forget[l], w_gla_up[l], b_gla[l], g_gla[l], conv_w[l],
              w_branch_a[l], w_branch_b[l], w_branch_c[l], w_out[l])
        gla0 = jnp.zeros((bp, B_HEADS, B_KEY_DIM, B_VAL_DIM), jnp.float32)
        conv0 = jnp.zeros((bp, CONV_W - 1, D_C), xp.dtype)
        xp, kp, vp, fp, gp, cp = hybrid_layer(xp, None, gla0, conv0, *lw)
        k_past = cache_k[l][page_table].reshape(bs, past_len, A_HEADS, A_HEAD_DIM)
        v_past = cache_v[l][page_table].reshape(bs, past_len, A_HEADS, A_HEAD_DIM)
        f_past = cache_logf[l][page_table].reshape(bs, past_len, A_HEADS)
        xs, ksn, vsn, fsn, gsn, csn = hybrid_layer(xs, (k_past, v_past, f_past), state_gla[l], state_conv[l], *lw)
        kp_l.append(kp); vp_l.append(vp); fp_l.append(fp)
        ks_l.append(ksn); vs_l.append(vsn); fs_l.append(fsn)
        gp_l.append(gp); gs_l.append(gsn); cp_l.append(cp); cs_l.append(csn)
    y_prompt = rms_norm(xp, g_final)
    y_sample = rms_norm(xs, g_final)
    return (y_prompt, y_sample,
            jnp.stack(kp_l), jnp.stack(vp_l), jnp.stack(fp_l),
            jnp.stack(ks_l), jnp.stack(vs_l), jnp.stack(fs_l),
            jnp.stack(gp_l), jnp.stack(gs_l),
            jnp.stack(cp_l), jnp.stack(cs_l))
```

```python
import functools

import jax
import jax.numpy as jnp
import numpy as np
from jax import lax
from jax.experimental import pallas as pl
from jax.experimental.pallas import tpu as pltpu

F32 = jnp.float32
BF16 = jnp.bfloat16

D_MODEL = 1024
A_HEADS = 8
A_HEAD_DIM = 64
D_A = 512
B_HEADS = 4
B_KEY_DIM = 64
B_VAL_DIM = 128
D_BK = 256
D_B = 512
GLA_LOW_RANK = 16
GLA_TAU = 16.0
GLA_CHUNK = 64
D_C = 512
CONV_W = 3
PAGE_SIZE = 128
EPS = 1e-6
SUB = 8
LANES = 128

COL_MG = 0
COL_QA = 3072
COL_GA = 3584
COL_QB = 4096
COL_KB = 4352
COL_VB = 4608
COL_GB = 5120
COL_XC = 5632
COL_BC = 6144
COL_CC = 6656
COL_GC = 7168
COL_KA = 7680
COL_VA = 8192
N_MAIN_PROMPT = 7680
N_MAIN_SAMPLE = 8704
VMEM_LIMIT = 48 * 1024 * 1024


def _dot(a, b):
    return jnp.dot(a, b, preferred_element_type=F32)


def _dot_nt(a, b):
    return lax.dot_general(a, b, (((1,), (1,)), ((), ())), preferred_element_type=F32)


def _dot_tn(a, b):
    return lax.dot_general(a, b, (((0,), (0,)), ((), ())), preferred_element_type=F32)


def _split3(x):
    hi = x.astype(BF16)
    r = x - hi.astype(F32)
    mid = r.astype(BF16)
    lo = (r - mid.astype(F32)).astype(BF16)
    return hi, mid, lo


def _dot3_right(m01, x):
    hi, mid, lo = _split3(x)
    return _dot(m01, hi) + _dot(m01, mid) + _dot(m01, lo)


def _dot3_left(x, m01):
    hi, mid, lo = _split3(x)
    return _dot(hi, m01) + _dot(mid, m01) + _dot(lo, m01)


def _log_sigmoid(x):
    return jnp.minimum(x, 0.0) - jnp.log1p(jnp.exp(-jnp.abs(x)))


def _silu(x):
    return x * jax.nn.sigmoid(x)


def _proj_kernel(*refs, transposed_kv):
    if transposed_kv:
        (x_ref, g_ref, wm_ref, ws_ref, wkv_ref, wfa_ref,
         z_ref, zs_ref, kt_ref, vt_ref, fat_ref, hn_ref) = refs
    else:
        x_ref, g_ref, wm_ref, ws_ref, z_ref, zs_ref, hn_ref = refs
    j = pl.program_id(1)

    @pl.when(j == 0)
    def _():
        x = x_ref[...]
        ms = jnp.mean(x * x, axis=-1, keepdims=True)
        hn = (x * lax.rsqrt(ms + EPS) * g_ref[...]).astype(BF16)
        hn_ref[...] = hn
        zs_ref[...] = _dot(hn, ws_ref[...])
        if transposed_kv:
            kv = _dot_nt(wkv_ref[...], hn)
            kt_ref[...] = kv[:D_A]
            vt_ref[...] = kv[D_A:]
            fat_ref[...] = _dot_nt(wfa_ref[...], hn)[:A_HEADS]

    z_ref[...] = _dot(hn_ref[...], wm_ref[...])


def _proj(x2d, g, wm, ws, wkv_t=None, wfa_t=None, *, tm, tn, seq_len=None):
    n = x2d.shape[0]
    n_main = wm.shape[1]
    transposed_kv = wkv_t is not None
    grid = (n // tm, n_main // tn)
    in_specs = [
        pl.BlockSpec((tm, D_MODEL), lambda i, j: (i, 0)),
        pl.BlockSpec((1, D_MODEL), lambda i, j: (0, 0)),
        pl.BlockSpec((D_MODEL, tn), lambda i, j: (0, j)),
        pl.BlockSpec((D_MODEL, LANES), lambda i, j: (0, 0)),
    ]
    out_shape = [jax.ShapeDtypeStruct((n, n_main), F32), jax.ShapeDtypeStruct((n, LANES), F32)]
    out_specs = [
        pl.BlockSpec((tm, tn), lambda i, j: (i, j)),
        pl.BlockSpec((tm, LANES), lambda i, j: (i, 0)),
    ]
    args = [x2d, g, wm, ws]
    if transposed_kv:
        tiles_per_seq = seq_len // tm
        bsz = n // seq_len
        in_specs += [
            pl.BlockSpec((2 * D_A, D_MODEL), lambda i, j: (0, 0)),
            pl.BlockSpec((2 * A_HEADS, D_MODEL), lambda i, j: (0, 0)),
        ]
        args += [wkv_t, wfa_t]
        seq_map = lambda i, j: (i // tiles_per_seq, 0, i % tiles_per_seq)
        out_shape += [jax.ShapeDtypeStruct((bsz, D_A, seq_len), F32),
                      jax.ShapeDtypeStruct((bsz, D_A, seq_len), F32),
                      jax.ShapeDtypeStruct((bsz, A_HEADS, seq_len), F32)]
        out_specs += [pl.BlockSpec((None, D_A, tm), seq_map),
                      pl.BlockSpec((None, D_A, tm), seq_map),
                      pl.BlockSpec((None, A_HEADS, tm), seq_map)]
    return pl.pallas_call(
        functools.partial(_proj_kernel, transposed_kv=transposed_kv),
        grid=grid,
        in_specs=in_specs,
        out_specs=out_specs,
        out_shape=out_shape,
        scratch_shapes=[pltpu.VMEM((tm, D_MODEL), BF16)],
        compiler_params=pltpu.CompilerParams(
            dimension_semantics=("parallel", "arbitrary"), vmem_limit_bytes=VMEM_LIMIT),
        name="proj_t" if transposed_kv else "proj",
    )(*args)


def _fgate_kernel(fat_ref, bf_ref, triu_ref, lf_ref, cum_ref, *, seq_len):
    lf = _log_sigmoid(fat_ref[...] + bf_ref[...])
    lf_ref[...] = lf
    n_chunks = seq_len // LANES
    stacked = jnp.concatenate([lf[:, c * LANES:(c + 1) * LANES] for c in range(n_chunks)], axis=0)
    local = _dot3_left(stacked, triu_ref[...])
    carry = jnp.zeros((A_HEADS, 1), F32)
    for c in range(n_chunks):
        cum = local[c * A_HEADS:(c + 1) * A_HEADS] + carry
        cum_ref[:, c * LANES:(c + 1) * LANES] = cum
        carry = cum[:, LANES - 1:LANES]


def _fgate(fat, bf_col, triu):
    bsz, _, seq_len = fat.shape
    spec = pl.BlockSpec((None, A_HEADS, seq_len), lambda b: (b, 0, 0))
    return pl.pallas_call(
        functools.partial(_fgate_kernel, seq_len=seq_len),
        grid=(bsz,),
        in_specs=[spec,
                  pl.BlockSpec((A_HEADS, 1), lambda b: (0, 0)),
                  pl.BlockSpec((LANES, LANES), lambda b: (0, 0))],
        out_specs=[spec, spec],
        out_shape=[jax.ShapeDtypeStruct(fat.shape, F32)] * 2,
        compiler_params=pltpu.CompilerParams(dimension_semantics=("parallel",)),
        name="fgate",
    )(fat, bf_col, triu)


def _fox_prompt_kernel(q_ref, kt_ref, vt_ref, fk_ref, ga_ref, o_ref, m_ref, l_ref, acc_ref, *, tq, tk):
    i = pl.program_id(2)
    j = pl.program_id(3)
    nk = pl.num_programs(3)

    @pl.when(j == 0)
    def _():
        m_ref[...] = jnp.full(m_ref.shape, -jnp.inf, F32)
        l_ref[...] = jnp.zeros(l_ref.shape, F32)
        acc_ref[...] = jnp.zeros(acc_ref.shape, F32)

    @pl.when(j <= i)
    def _():
        q = q_ref[...] * (A_HEAD_DIM ** -0.5)
        lane = lax.broadcasted_iota(jnp.int32, q.shape, 1)
        kt = kt_ref[...].astype(BF16)
        vt = vt_ref[...].astype(BF16)
        fk = fk_ref[...]
        row = i * tq + lax.broadcasted_iota(jnp.int32, (tq, tk), 0)
        col = j * tk + lax.broadcasted_iota(jnp.int32, (tq, tk), 1)
        causal = col <= row
        for hh in range(2):
            in_head = (lane >= hh * A_HEAD_DIM) & (lane < (hh + 1) * A_HEAD_DIM)
            qm = jnp.where(in_head, q, 0.0).astype(BF16)
            s = _dot(qm, kt) - fk[hh:hh + 1, :]
            s = jnp.where(causal, s, -jnp.inf)
            m_prev = m_ref[hh]
            m_new = jnp.maximum(m_prev, jnp.max(s, axis=-1, keepdims=True))
            alpha = jnp.exp(m_prev - m_new)
            p = jnp.exp(s - m_new)
            l_ref[hh] = alpha * l_ref[hh] + jnp.sum(p, axis=-1, keepdims=True)
            acc_ref[hh] = alpha * acc_ref[hh] + _dot_nt(p.astype(BF16), vt)
            m_ref[hh] = m_new

    @pl.when(j == nk - 1)
    def _():
        lane = lax.broadcasted_iota(jnp.int32, (tq, LANES), 1)
        o = jnp.where(lane < A_HEAD_DIM, acc_ref[0] / l_ref[0], acc_ref[1] / l_ref[1])
        o_ref[...] = (o * _silu(ga_ref[...])).astype(BF16)


def _fox_prompt(z, kt, vt, cum, *, bsz, seq_len, tq, tk):
    nq, nk = seq_len // tq, seq_len // tk
    n_pairs = A_HEADS // 2
    cum4 = cum.reshape(bsz, n_pairs, 2, seq_len)
    kv_map = lambda b, p, i, j: (b, p, jnp.minimum(j, i))
    return pl.pallas_call(
        functools.partial(_fox_prompt_kernel, tq=tq, tk=tk),
        grid=(bsz, n_pairs, nq, nk),
        in_specs=[
            pl.BlockSpec((tq, LANES), lambda b, p, i, j: (b * nq + i, COL_QA // LANES + p)),
            pl.BlockSpec((None, LANES, tk), kv_map),
            pl.BlockSpec((None, LANES, tk), kv_map),
            pl.BlockSpec((None, None, 2, tk), lambda b, p, i, j: (b, p, 0, jnp.minimum(j, i))),
            pl.BlockSpec((tq, LANES), lambda b, p, i, j: (b * nq + i, COL_GA // LANES + p)),
        ],
        out_specs=pl.BlockSpec((tq, LANES), lambda b, p, i, j: (b * nq + i, p)),
        out_shape=jax.ShapeDtypeStruct((bsz * seq_len, D_A), BF16),
        scratch_shapes=[pltpu.VMEM((2, tq, 1), F32), pltpu.VMEM((2, tq, 1), F32),
                        pltpu.VMEM((2, tq, LANES), F32)],
        compiler_params=pltpu.CompilerParams(
            dimension_semantics=("parallel", "parallel", "parallel", "arbitrary"),
            vmem_limit_bytes=VMEM_LIMIT),
        name="fox_prompt",
    )(z, kt, vt, cum4, z)


def _fox_sample_kernel(pt_ref, *refs, n_group, dec_seq):
    del pt_ref
    q_ref, zs_ref, kn_ref, vn_ref, ga_ref, bf_ref, triu_ref, tril_ref, sel_ref = refs[:9]
    kt_refs = refs[9:9 + n_group]
    vt_refs = refs[9 + n_group:9 + 2 * n_group]
    lf_refs = refs[9 + 2 * n_group:9 + 3 * n_group]
    o_ref, lfn_ref, m_ref, l_ref, acc_ref, off_ref, qbd_ref = refs[9 + 3 * n_group:]
    j = pl.program_id(1)
    nj = pl.num_programs(1)
    rows = dec_seq * A_HEADS
    row_head = lax.broadcasted_iota(jnp.int32, (rows, D_A), 0) % A_HEADS
    lane_head = lax.broadcasted_iota(jnp.int32, (rows, D_A), 1) // A_HEAD_DIM
    own_head = row_head == lane_head

    @pl.when(j == 0)
    def _():
        m_ref[...] = jnp.full(m_ref.shape, -jnp.inf, F32)
        l_ref[...] = jnp.zeros(l_ref.shape, F32)
        acc_ref[...] = jnp.zeros(acc_ref.shape, F32)
        off_ref[...] = jnp.zeros(off_ref.shape, F32)
        q = q_ref[...] * (A_HEAD_DIM ** -0.5)
        qrep = jnp.concatenate(
            [jnp.broadcast_to(q[t:t + 1, :], (A_HEADS, D_A)) for t in range(dec_seq)], axis=0)
        qbd_ref[...] = jnp.where(own_head, qrep, 0.0).astype(BF16)

    qbd = qbd_ref[...]
    m = m_ref[...]
    l = l_ref[...]
    acc = acc_ref[...]
    off = off_ref[...]
    lf_pages = jnp.concatenate([lf_refs[g][...] for g in range(n_group)], axis=0)
    local = _dot3_left(lf_pages, triu_ref[...])
    for g in range(n_group):
        fk = off + local[g * A_HEADS:(g + 1) * A_HEADS]
        off = fk[:, LANES - 1:LANES]
        s = _dot(qbd, kt_refs[g][...].astype(BF16)) - jnp.concatenate([fk] * dec_seq, axis=0)
        m_new = jnp.maximum(m, jnp.max(s, axis=-1, keepdims=True))
        alpha = jnp.exp(m - m_new)
        p = jnp.exp(s - m_new)
        l = alpha * l + jnp.sum(p, axis=-1, keepdims=True)
        acc = alpha * acc + _dot_nt(p.astype(BF16), vt_refs[g][...].astype(BF16))
        m = m_new
    m_ref[...] = m
    l_ref[...] = l
    acc_ref[...] = acc
    off_ref[...] = off

    @pl.when(j == nj - 1)
    def _():
        lane8 = lax.broadcasted_iota(jnp.int32, (dec_seq, LANES), 1)
        lfn = jnp.where(lane8 < A_HEADS, _log_sigmoid(zs_ref[...] + bf_ref[...]), 0.0)
        lfn_ref[...] = lfn[:, :A_HEADS]
        pad = jnp.zeros((LANES - dec_seq, LANES), F32)
        cum = _dot3_right(tril_ref[...], jnp.concatenate([lfn, pad], axis=0))
        hi, mid, lo = _split3(cum)
        sel = sel_ref[...]
        fk_new = _dot_nt(sel, hi) + _dot_nt(sel, mid) + _dot_nt(sel, lo)
        fk_new = fk_new + jnp.concatenate([off] * dec_seq, axis=0)
        kpad = jnp.zeros((LANES - dec_seq, D_A), F32)
        kn = jnp.concatenate([kn_ref[...], kpad], axis=0).astype(BF16)
        vn = jnp.concatenate([vn_ref[...], kpad], axis=0).astype(BF16)
        s = _dot_nt(qbd, kn) - fk_new
        key = lax.broadcasted_iota(jnp.int32, (rows, LANES), 1)
        tok = lax.broadcasted_iota(jnp.int32, (rows, LANES), 0) // A_HEADS
        s = jnp.where(key <= tok, s, -jnp.inf)
        m_new = jnp.maximum(m, jnp.max(s, axis=-1, keepdims=True))
        alpha = jnp.exp(m - m_new)
        p = jnp.exp(s - m_new)
        l_fin = alpha * l + jnp.sum(p, axis=-1, keepdims=True)
        acc_fin = alpha * acc + _dot(p.astype(BF16), vn)
        o = jnp.where(own_head, acc_fin / l_fin, 0.0)
        o = jnp.sum(o.reshape(dec_seq, A_HEADS, D_A), axis=1)
        o_ref[...] = (o * _silu(ga_ref[...])).astype(BF16)


def _fox_sample(pt_flat, z, zs, kt_pool, vt_pool, lf_pool, bf_row, triu, tril, sel, *,
                bsz, dec_seq, n_pages, n_group):
    rows = dec_seq * A_HEADS
    steps = n_pages // n_group

    def page_map(g):
        return lambda b, j, pt: (pt[b * n_pages + j * n_group + g], 0, 0)

    row_blk = lambda col: pl.BlockSpec((dec_seq, D_A), lambda b, j, pt: (b, col // D_A))
    const = lambda shape: pl.BlockSpec(shape, lambda b, j, pt: (0,) * len(shape))
    in_specs = [
        row_blk(COL_QA),
        pl.BlockSpec((dec_seq, LANES), lambda b, j, pt: (b, 0)),
        row_blk(COL_KA), row_blk(COL_VA), row_blk(COL_GA),
        const((1, LANES)), const((LANES, LANES)), const((LANES, LANES)), const((rows, LANES)),
    ]
    in_specs += [pl.BlockSpec((None, D_A, PAGE_SIZE), page_map(g)) for g in range(n_group)]
    in_specs += [pl.BlockSpec((None, D_A, PAGE_SIZE), page_map(g)) for g in range(n_group)]
    in_specs += [pl.BlockSpec((None, A_HEADS, PAGE_SIZE), page_map(g)) for g in range(n_group)]
    grid_spec = pltpu.PrefetchScalarGridSpec(
        num_scalar_prefetch=1,
        grid=(bsz, steps),
        in_specs=in_specs,
        out_specs=[pl.BlockSpec((dec_seq, D_A), lambda b, j, pt: (b, 0)),
                   pl.BlockSpec((dec_seq, A_HEADS), lambda b, j, pt: (b, 0))],
        scratch_shapes=[pltpu.VMEM((rows, 1), F32), pltpu.VMEM((rows, 1), F32),
                        pltpu.VMEM((rows, D_A), F32), pltpu.VMEM((A_HEADS, 1), F32),
                        pltpu.VMEM((rows, D_A), BF16)],
    )
    return pl.pallas_call(
        functools.partial(_fox_sample_kernel, n_group=n_group, dec_seq=dec_seq),
        grid_spec=grid_spec,
        out_shape=[jax.ShapeDtypeStruct((bsz * dec_seq, D_A), BF16),
                   jax.ShapeDtypeStruct((bsz * dec_seq, A_HEADS), F32)],
        compiler_params=pltpu.CompilerParams(
            dimension_semantics=("parallel", "arbitrary"), vmem_limit_bytes=VMEM_LIMIT),
        name="fox_sample",
    )(pt_flat, z, zs, z, z, z, bf_row, triu, tril, sel,
      *([kt_pool] * n_group), *([vt_pool] * n_group), *([lf_pool] * n_group))


def _gla_tables(chunk):
    n_slots = chunk // SUB - 1
    t = np.arange(chunk)[:, None]
    r = np.arange(chunk)[None, :]
    blk_start = (t // SUB) * SUB
    sections = [((r >= blk_start) & (r <= t))]
    for i in range(1, n_slots + 1):
        ref = SUB * i - 1
        sections.append(((r > t) & (r <= ref) & (t <= ref)))
    sections.append(r <= t)
    sections.append(r > t)
    return np.concatenate(sections, axis=0).astype(np.float32), n_slots


def _gla_kernel(q_ref, k_ref, v_ref, gb_ref, zs_ref, s0_ref, wup_ref, bg_ref, gg_ref, tab_ref, ind_ref,
                y_ref, sout_ref, sbd_ref, *, chunk, tokens, n_slots):
    c = pl.program_id(1)
    nc = pl.num_programs(1)
    bd_mask = (lax.broadcasted_iota(jnp.int32, (D_BK, D_B), 0) // B_KEY_DIM
               == lax.broadcasted_iota(jnp.int32, (D_BK, D_B), 1) // B_VAL_DIM)

    @pl.when(c == 0)
    def _():
        sbd_ref[...] = jnp.zeros(sbd_ref.shape, F32)
        for h in range(B_HEADS):
            sbd_ref[h * B_KEY_DIM:(h + 1) * B_KEY_DIM, h * B_VAL_DIM:(h + 1) * B_VAL_DIM] = s0_ref[h]

    def pad_rows(x):
        if tokens == chunk:
            return x
        return jnp.concatenate([x, jnp.zeros((chunk - tokens, x.shape[1]), x.dtype)], axis=0)

    q = pad_rows(q_ref[...]) * (B_KEY_DIM ** -0.5)
    k = pad_rows(k_ref[...])
    v = pad_rows(v_ref[...])
    a_logit = _dot(pad_rows(zs_ref[...]).astype(BF16), wup_ref[...]) + bg_ref[...]
    log_a = _log_sigmoid(a_logit) * (1.0 / GLA_TAU)
    if tokens != chunk:
        log_a = jnp.where(lax.broadcasted_iota(jnp.int32, log_a.shape, 0) < tokens, log_a, 0.0)
    n_blocks = -(-tokens // SUB)

    la_hi, la_mid, la_lo = _split3(log_a)
    tab = tab_ref[...]
    expo = _dot(tab, la_hi) + _dot(tab, la_mid) + _dot(tab, la_lo)
    sec = lambda i: expo[i * chunk:(i + 1) * chunk]
    lq = sec(0)
    q_blk = q * jnp.exp(lq)
    row_blk = lax.broadcasted_iota(jnp.int32, (chunk, D_BK), 0) // SUB
    row_pos = lax.broadcasted_iota(jnp.int32, (chunk, D_BK), 0)
    v_bf = v.astype(BF16)

    o = _dot((q * jnp.exp(sec(n_slots + 1))).astype(BF16), sbd_ref[...].astype(BF16))

    if n_slots > 0:
        head_rows = lax.broadcasted_iota(jnp.int32, (B_HEADS * chunk, D_BK), 0) // chunk
        head_lane = lax.broadcasted_iota(jnp.int32, (B_HEADS * chunk, D_BK), 1) // B_KEY_DIM
        lhs, rhs = [], []
        for i in range(1, n_slots + 1):
            lhs.append(jnp.where(row_blk == i, q_blk, 0.0).astype(BF16))
            k_i = jnp.where(row_pos < SUB * i, k * jnp.exp(sec(i)), 0.0)
            rhs.append(jnp.where(head_rows == head_lane, jnp.concatenate([k_i] * B_HEADS, axis=0), 0.0).astype(BF16))
        att = _dot_nt(jnp.concatenate(lhs, axis=1), jnp.concatenate(rhs, axis=1))
        v_rows = lax.broadcasted_iota(jnp.int32, (B_HEADS * chunk, D_B), 0) // chunk
        v_lane = lax.broadcasted_iota(jnp.int32, (B_HEADS * chunk, D_B), 1) // B_VAL_DIM
        v_bd = jnp.where(v_rows == v_lane, jnp.concatenate([v] * B_HEADS, axis=0), 0.0).astype(BF16)
        o = o + _dot(att.astype(BF16), v_bd)

    t_in = lax.broadcasted_iota(jnp.int32, (SUB, D_BK), 0)
    prods = []
    for b in range(n_blocks):
        qb = q[b * SUB:(b + 1) * SUB]
        lb = lq[b * SUB:(b + 1) * SUB]
        for s in range(SUB):
            r = b * SUB + s
            valid = t_in >= s
            e = jnp.exp(jnp.where(valid, lb - lq[r:r + 1], 0.0))
            prods.append(jnp.where(valid, qb * k[r:r + 1] * e, 0.0))
    w = _dot(jnp.concatenate(prods, axis=0).astype(BF16), ind_ref[...])
    o_diag = []
    for b in range(n_blocks):
        acc = jnp.zeros((SUB, D_B), F32)
        for s in range(SUB):
            r = b * SUB + s
            acc = acc + w[r * SUB:(r + 1) * SUB] * v[r:r + 1]
        o_diag.append(acc)
    if n_blocks * SUB < chunk:
        o_diag.append(jnp.zeros((chunk - n_blocks * SUB, D_B), F32))
    o = o + jnp.concatenate(o_diag, axis=0)

    ones = jnp.ones((chunk, LANES), BF16)
    tot = _dot_tn(la_hi, ones) + _dot_tn(la_mid, ones) + _dot_tn(la_lo, ones)
    decay = jnp.concatenate([jnp.exp(tot)] * B_HEADS, axis=1)
    k_end = (k * jnp.exp(sec(n_slots + 2))).astype(BF16)
    upd = _dot_tn(k_end, v_bf)
    sbd_ref[...] = decay * sbd_ref[...] + jnp.where(bd_mask, upd, 0.0)

    gb = gb_ref[...]
    ys = []
    for h in range(B_HEADS):
        sl = slice(h * B_VAL_DIM, (h + 1) * B_VAL_DIM)
        oh = o[:tokens, sl]
        ms = jnp.mean(oh * oh, axis=-1, keepdims=True)
        ys.append(oh * lax.rsqrt(ms + EPS) * gg_ref[:, sl] * _silu(gb[:, sl]))
    y_ref[...] = jnp.concatenate(ys, axis=1).astype(BF16)

    @pl.when(c == nc - 1)
    def _():
        for h in range(B_HEADS):
            sout_ref[h] = sbd_ref[h * B_KEY_DIM:(h + 1) * B_KEY_DIM, h * B_VAL_DIM:(h + 1) * B_VAL_DIM]


def _gla(z, zs, s0, wup_pad, bg_row, gg_row, *, bsz, seq_len):
    chunk = GLA_CHUNK
    tokens = min(seq_len, chunk)
    tab_np, n_slots = _gla_tables(chunk)
    tab = jnp.asarray(tab_np, BF16)
    ind_np = (np.arange(D_BK)[:, None] // B_KEY_DIM == np.arange(D_B)[None, :] // B_VAL_DIM)
    ind = jnp.asarray(ind_np.astype(np.float32), BF16)
    nc = seq_len // tokens
    tok = lambda width, col: pl.BlockSpec((tokens, width), lambda b, c: (b * nc + c, col // width))
    const = lambda shape: pl.BlockSpec(shape, lambda b, c: (0,) * len(shape))
    state_spec = pl.BlockSpec((None, B_HEADS, B_KEY_DIM, B_VAL_DIM), lambda b, c: (b, 0, 0, 0))
    return pl.pallas_call(
        functools.partial(_gla_kernel, chunk=chunk, tokens=tokens, n_slots=n_slots),
        grid=(bsz, nc),
        in_specs=[tok(D_BK, COL_QB), tok(D_BK, COL_KB), tok(D_B, COL_VB), tok(D_B, COL_GB),
                  pl.BlockSpec((tokens, LANES), lambda b, c: (b * nc + c, 0)),
                  state_spec,
                  const((LANES, D_BK)), const((1, D_BK)), const((1, D_B)),
                  const(tab.shape), const(ind.shape)],
        out_specs=[pl.BlockSpec((tokens, D_B), lambda b, c: (b * nc + c, 0)), state_spec],
        out_shape=[jax.ShapeDtypeStruct((bsz * seq_len, D_B), BF16),
                   jax.ShapeDtypeStruct((bsz, B_HEADS, B_KEY_DIM, B_VAL_DIM), F32)],
        scratch_shapes=[pltpu.VMEM((D_BK, D_B), F32)],
        compiler_params=pltpu.CompilerParams(
            dimension_semantics=("parallel", "arbitrary"), vmem_limit_bytes=VMEM_LIMIT),
        name="gla",
    )(z, z, z, z, zs, s0, wup_pad, bg_row, gg_row, tab, ind)


def _conv_kernel(xc_ref, bc_ref, cc_ref, gc_ref, buf_ref, w_ref, y_ref, new_ref, ext_ref, *, tt):
    i = pl.program_id(1)
    halo = CONV_W - 1

    @pl.when(i == 0)
    def _():
        ext_ref[SUB - halo:SUB, :] = buf_ref[...]

    @pl.when(i > 0)
    def _():
        ext_ref[SUB - halo:SUB, :] = ext_ref[SUB + tt - halo:SUB + tt, :]

    ext_ref[SUB:SUB + tt, :] = cc_ref[...] * xc_ref[...]
    w = w_ref[...]
    y = w[0:1] * ext_ref[SUB - 2:SUB - 2 + tt, :]
    y = y + w[1:2] * ext_ref[SUB - 1:SUB - 1 + tt, :]
    y = y + w[2:3] * ext_ref[SUB:SUB + tt, :]
    y_ref[...] = (bc_ref[...] * y * _silu(gc_ref[...])).astype(BF16)
    new_ref[...] = ext_ref[SUB + tt - halo:SUB + tt, :]


def _conv(z, buf, w, *, bsz, seq_len):
    tt = min(seq_len, 512)
    nt = seq_len // tt
    tok = lambda col: pl.BlockSpec((tt, D_C), lambda b, i: (b * nt + i, col // D_C))
    state_spec = pl.BlockSpec((None, CONV_W - 1, D_C), lambda b, i: (b, 0, 0))
    return pl.pallas_call(
        functools.partial(_conv_kernel, tt=tt),
        grid=(bsz, nt),
        in_specs=[tok(COL_XC), tok(COL_BC), tok(COL_CC), tok(COL_GC), state_spec,
                  pl.BlockSpec((CONV_W, D_C), lambda b, i: (0, 0))],
        out_specs=[pl.BlockSpec((tt, D_C), lambda b, i: (b * nt + i, 0)), state_spec],
        out_shape=[jax.ShapeDtypeStruct((bsz * seq_len, D_C), BF16),
                   jax.ShapeDtypeStruct((bsz, CONV_W - 1, D_C), F32)],
        scratch_shapes=[pltpu.VMEM((SUB + tt, D_C), F32)],
        compiler_params=pltpu.CompilerParams(dimension_semantics=("parallel", "arbitrary")),
        name="conv",
    )(z, z, z, z, buf, w)


def _merge_kernel(ya_ref, yb_ref, yc_ref, g0_ref, g1_ref, g2_ref, x_ref, wa_ref, wb_ref, wc_ref, wo_ref,
                  gf_ref, o_ref, *, final_norm):
    merged = (jax.nn.sigmoid(g0_ref[...]) * _dot(ya_ref[...], wa_ref[...])
              + jax.nn.sigmoid(g1_ref[...]) * _dot(yb_ref[...], wb_ref[...])
              + jax.nn.sigmoid(g2_ref[...]) * _dot(yc_ref[...], wc_ref[...]))
    out = x_ref[...] + _dot(merged.astype(BF16), wo_ref[...])
    if final_norm:
        ms = jnp.mean(out * out, axis=-1, keepdims=True)
        out = out * lax.rsqrt(ms + EPS) * gf_ref[...]
    o_ref[...] = out


def _merge_out(ya, yb, yc, z, x2d, wa, wb, wc, wo, gf_row, *, tm, final_norm):
    n = x2d.shape[0]
    tok = lambda width: pl.BlockSpec((tm, width), lambda i: (i, 0))
    gate = lambda k: pl.BlockSpec((tm, D_MODEL), lambda i: (i, COL_MG // D_MODEL + k))
    const = lambda shape: pl.BlockSpec(shape, lambda i: (0,) * len(shape))
    return pl.pallas_call(
        functools.partial(_merge_kernel, final_norm=final_norm),
        grid=(n // tm,),
        in_specs=[tok(D_A), tok(D_B), tok(D_C), gate(0), gate(1), gate(2), tok(D_MODEL),
                  const((D_A, D_MODEL)), const((D_B, D_MODEL)), const((D_C, D_MODEL)),
                  const((D_MODEL, D_MODEL)), const((1, D_MODEL))],
        out_specs=tok(D_MODEL),
        out_shape=jax.ShapeDtypeStruct((n, D_MODEL), F32),
        compiler_params=pltpu.CompilerParams(
            dimension_semantics=("parallel",), vmem_limit_bytes=VMEM_LIMIT),
        name="merge_out",
    )(ya, yb, yc, z, z, z, x2d, wa, wb, wc, wo, gf_row)


def _split_w_in(w):
    sizes = (D_A, D_A, D_A, A_HEADS, D_A, D_BK, D_BK, D_B, GLA_LOW_RANK, D_B, D_C, D_C, D_C, D_C, 3 * D_MODEL)
    offs = np.concatenate([[0], np.cumsum(sizes)])
    part = lambda i: w[:, int(offs[i]):int(offs[i + 1])]
    qa, ka, va, fa, ga, qb, kb, vb, lrb, gb, xc, bc, cc, gc, mg = [part(i) for i in range(len(sizes))]
    main = jnp.concatenate([mg, qa, ga, qb, kb, vb, gb, xc, bc, cc, gc], axis=1).astype(BF16)
    kv = jnp.concatenate([ka, va], axis=1).astype(BF16)
    small = jnp.concatenate(
        [fa, lrb, jnp.zeros((D_MODEL, LANES - A_HEADS - GLA_LOW_RANK), w.dtype)], axis=1).astype(BF16)
    return main, kv, small, fa.astype(BF16)


def kernel(x_prompt, x_sample, cache_k, cache_v, cache_logf, state_gla, state_conv, page_table, g_norm, w_in, b_forget, w_gla_up, b_gla, g_gla, conv_w, w_branch_a, w_branch_b, w_branch_c, w_out, g_final):
    depth = w_in.shape[0]
    bp, seq, _ = x_prompt.shape
    bs, dec_seq, _ = x_sample.shape
    n_pool = cache_k.shape[1]
    n_pages = page_table.shape[1]
    assert dec_seq == SUB and cache_k.shape[2] == PAGE_SIZE

    kt_pool = jnp.transpose(cache_k, (0, 1, 3, 4, 2)).reshape(depth * n_pool, D_A, PAGE_SIZE)
    vt_pool = jnp.transpose(cache_v, (0, 1, 3, 4, 2)).reshape(depth * n_pool, D_A, PAGE_SIZE)
    lf_pool = jnp.transpose(cache_logf, (0, 1, 3, 2)).reshape(depth * n_pool, A_HEADS, PAGE_SIZE)

    idx = np.arange(LANES)
    triu = jnp.asarray((idx[:, None] <= idx[None, :]).astype(np.float32), BF16)
    tril = jnp.asarray((idx[:, None] >= idx[None, :]).astype(np.float32), BF16)
    rows = dec_seq * A_HEADS
    sel = jnp.asarray((np.arange(rows)[:, None] % A_HEADS == idx[None, :]).astype(np.float32), BF16)

    xp = x_prompt.reshape(bp * seq, D_MODEL)
    xs = x_sample.reshape(bs * dec_seq, D_MODEL)
    zero_gla = jnp.zeros((bp, B_HEADS, B_KEY_DIM, B_VAL_DIM), F32)
    zero_conv = jnp.zeros((bp, CONV_W - 1, D_C), F32)
    gf_row = g_final.reshape(1, D_MODEL)

    kp_l, vp_l, fp_l, ks_l, vs_l, fs_l, gp_l, gs_l, cp_l, cs_l = ([] for _ in range(10))
    for l in range(depth):
        w_main, w_kv, w_small, w_fa = _split_w_in(w_in[l])
        g_row = g_norm[l].reshape(1, D_MODEL)
        wup_pad = jnp.zeros((LANES, D_BK), F32).at[A_HEADS:A_HEADS + GLA_LOW_RANK].set(w_gla_up[l]).astype(BF16)
        bg_row = b_gla[l].reshape(1, D_BK)
        gg_row = g_gla[l].reshape(1, D_B)
        wa, wb, wc, wo = (w[l].astype(BF16) for w in (w_branch_a, w_branch_b, w_branch_c, w_out))
        final = l == depth - 1

        wfa_t = jnp.concatenate([w_fa.T, jnp.zeros((A_HEADS, D_MODEL), BF16)], axis=0)
        z, zs, kt, vt, fat = _proj(xp, g_row, w_main, w_small, w_kv.T, wfa_t, tm=1024, tn=512, seq_len=seq)
        lf_t, cum_t = _fgate(fat, b_forget[l].reshape(A_HEADS, 1), triu)
        ya = _fox_prompt(z, kt, vt, cum_t, bsz=bp, seq_len=seq, tq=512, tk=512)
        yb, gla_p = _gla(z, zs, zero_gla, wup_pad, bg_row, gg_row, bsz=bp, seq_len=seq)
        yc, conv_p = _conv(z, zero_conv, conv_w[l], bsz=bp, seq_len=seq)
        xp = _merge_out(ya, yb, yc, z, xp, wa, wb, wc, wo, gf_row, tm=512, final_norm=final)
        kp_l.append(kt); vp_l.append(vt); fp_l.append(lf_t); gp_l.append(gla_p); cp_l.append(conv_p)

        w_main_s = jnp.concatenate([w_main, w_kv], axis=1)
        z, zs = _proj(xs, g_row, w_main_s, w_small, tm=bs * dec_seq, tn=512)
        pt_flat = (page_table + l * n_pool).reshape(-1).astype(jnp.int32)
        bf_row = jnp.zeros((1, LANES), F32).at[0, :A_HEADS].set(b_forget[l])
        ya, lf_s = _fox_sample(pt_flat, z, zs, kt_pool, vt_pool, lf_pool, bf_row, triu, tril, sel,
                               bsz=bs, dec_seq=dec_seq, n_pages=n_pages, n_group=8)
        yb, gla_s = _gla(z, zs, state_gla[l], wup_pad, bg_row, gg_row, bsz=bs, seq_len=dec_seq)
        yc, conv_s = _conv(z, state_conv[l], conv_w[l], bsz=bs, seq_len=dec_seq)
        xs = _merge_out(ya, yb, yc, z, xs, wa, wb, wc, wo, gf_row, tm=bs * dec_seq, final_norm=final)
        ks_l.append(z[:, COL_KA:COL_KA + D_A].reshape(bs, dec_seq, A_HEADS, A_HEAD_DIM))
        vs_l.append(z[:, COL_VA:COL_VA + D_A].reshape(bs, dec_seq, A_HEADS, A_HEAD_DIM))
        fs_l.append(lf_s.reshape(bs, dec_seq, A_HEADS))
        gs_l.append(gla_s); cs_l.append(conv_s)

    def heads_last(t):
        return jnp.transpose(t.reshape(depth, bp, A_HEADS, A_HEAD_DIM, seq), (0, 1, 4, 2, 3))

    return (xp.reshape(bp, seq, D_MODEL), xs.reshape(bs, dec_seq, D_MODEL),
            heads_last(jnp.stack(kp_l)), heads_last(jnp.stack(vp_l)),
            jnp.transpose(jnp.stack(fp_l), (0, 1, 3, 2)),
            jnp.stack(ks_l), jnp.stack(vs_l), jnp.stack(fs_l),
            jnp.stack(gp_l), jnp.stack(gs_l), jnp.stack(cp_l), jnp.stack(cs_l))
```
